```python
import jax, jax.numpy as jnp
from jax import lax
import numpy as np

D_MODEL = 1024
BATCH = 16
SEQ = 4096
DEPTH = 4

CHUNK = 128
A_WIDTH = D_MODEL
A_HEADS = 8
A_HEAD_DIM = A_WIDTH // A_HEADS
B_WIDTH = D_MODEL
CONV_WIDTH = 31
FFN_HIDDEN = 4 * D_MODEL
N_BRANCH = 2
N_MOD = 6
IN_COLS = 2 * A_WIDTH + 2 * B_WIDTH + N_BRANCH * D_MODEL
EPS = 1e-6

kernel_name = "hybrid_gmlp_conformer_gated_adaln"


def rmsnorm(x, g):
    xf = x.astype(jnp.float32)
    y = xf * lax.rsqrt(jnp.mean(xf * xf, axis=-1, keepdims=True) + EPS)
    return (y * g.astype(jnp.float32)).astype(x.dtype)


def layernorm(x, g, b):
    xf = x.astype(jnp.float32)
    mu = jnp.mean(xf, axis=-1, keepdims=True)
    xc = xf - mu
    var = jnp.mean(xc * xc, axis=-1, keepdims=True)
    y = xc * lax.rsqrt(var + EPS) * g.astype(jnp.float32) + b.astype(jnp.float32)
    return y.astype(x.dtype)


def chunked_spatial_gating(u, v, ln_g, ln_b, w_s, b_s):
    bsz, t, _ = v.shape
    v = layernorm(v, ln_g, ln_b)
    v = v.reshape(bsz, t // CHUNK, CHUNK, A_HEADS, A_HEAD_DIM)
    causal = jnp.tril(jnp.ones((CHUNK, CHUNK), dtype=bool))
    w = jnp.where(causal[None], w_s, 0).astype(v.dtype)
    s = jnp.einsum('hij,bnjhd->bnihd', w, v) + b_s.T.astype(v.dtype)[None, None, :, :, None]
    return u * s.reshape(bsz, t, A_WIDTH)


def conformer_conv(p, conv_w, conv_b, ln_g, ln_b):
    a, g = jnp.split(p, 2, axis=-1)
    z = a * jax.nn.sigmoid(g)
    z = lax.conv_general_dilated(
        z, conv_w.astype(z.dtype), window_strides=(1,),
        padding=[(CONV_WIDTH - 1, 0)],
        dimension_numbers=('NWC', 'WIO', 'NWC'),
        feature_group_count=B_WIDTH) + conv_b.astype(z.dtype)
    z = layernorm(z, ln_g, ln_b)
    return jax.nn.silu(z)


def setup_inputs(seed: int = 0) -> dict:
    key = jax.random.key(seed)
    ks = jax.random.split(key, 24)
    f32 = jnp.float32
    L, D = DEPTH, D_MODEL

    def nrm(k, shape, scale):
        return jax.random.normal(k, shape, f32) * scale

    return {
        "x": nrm(ks[0], (BATCH, SEQ, D), 1.0),
        "c": nrm(ks[1], (BATCH, D), 1.0),
        "w_ada": nrm(ks[2], (L, D, N_MOD * D), 0.5 * D ** -0.5),
        "b_ada": nrm(ks[3], (L, N_MOD * D), 0.02),
        "norm1_g": 1.0 + nrm(ks[4], (L, D), 0.05),
        "w_in": nrm(ks[5], (L, D, IN_COLS), D ** -0.5),
        "a_ln_g": 1.0 + nrm(ks[6], (L, A_WIDTH), 0.05),
        "a_ln_b": nrm(ks[7], (L, A_WIDTH), 0.02),
        "a_ws": nrm(ks[8], (L, A_HEADS, CHUNK, CHUNK), CHUNK ** -0.5),
        "a_bs": 1.0 + nrm(ks[9], (L, A_HEADS, CHUNK), 0.1),
        "w_pa": nrm(ks[10], (L, A_WIDTH, D), A_WIDTH ** -0.5),
        "b_conv_w": nrm(ks[11], (L, CONV_WIDTH, 1, B_WIDTH), CONV_WIDTH ** -0.5),
        "b_conv_b": nrm(ks[12], (L, B_WIDTH), 0.02),
        "b_ln_g": 1.0 + nrm(ks[13], (L, B_WIDTH), 0.05),
        "b_ln_b": nrm(ks[14], (L, B_WIDTH), 0.02),
        "w_pb": nrm(ks[15], (L, B_WIDTH, D), B_WIDTH ** -0.5),
        "w_out": nrm(ks[16], (L, D, D), D ** -0.5),
        "norm2_g": 1.0 + nrm(ks[17], (L, D), 0.05),
        "w_ff1": nrm(ks[18], (L, D, FFN_HIDDEN), D ** -0.5),
        "w_ff2": nrm(ks[19], (L, FFN_HIDDEN, D), FFN_HIDDEN ** -0.5),
        "final_g": 1.0 + nrm(ks[20], (D,), 0.05),
    }


def reference(x, c, w_ada, b_ada, norm1_g, w_in, a_ln_g, a_ln_b, a_ws, a_bs, w_pa,
              b_conv_w, b_conv_b, b_ln_g, b_ln_b, w_pb, w_out, norm2_g,
              w_ff1, w_ff2, final_g):
    split_at = [A_WIDTH, 2 * A_WIDTH, 2 * A_WIDTH + 2 * B_WIDTH,
                2 * A_WIDTH + 2 * B_WIDTH + D_MODEL]
    c_act = jax.nn.silu(c)
    for l in range(DEPTH):
        mod = (c_act @ w_ada[l] + b_ada[l])[:, None, :]
        sh1, sc1, gt1, sh2, sc2, gt2 = jnp.split(mod, N_MOD, axis=-1)

        h = rmsnorm(x, norm1_g[l]) * (1 + sc1) + sh1
        proj = h @ w_in[l]
        u, v, p_b, g_a, g_b = jnp.split(proj, split_at, axis=-1)
        y_a = chunked_spatial_gating(u, v, a_ln_g[l], a_ln_b[l], a_ws[l], a_bs[l]) @ w_pa[l]
        y_b = conformer_conv(p_b, b_conv_w[l], b_conv_b[l], b_ln_g[l], b_ln_b[l]) @ w_pb[l]
        merged = jax.nn.sigmoid(g_a) * y_a + jax.nn.sigmoid(g_b) * y_b
        x = x + gt1 * (merged @ w_out[l])

        h = rmsnorm(x, norm2_g[l]) * (1 + sc2) + sh2
        x = x + gt2 * (jnp.square(jax.nn.relu(h @ w_ff1[l])) @ w_ff2[l])

    return rmsnorm(x, final_g)
```

```python
import functools

import jax
import jax.numpy as jnp
from jax import lax
from jax.experimental import pallas as pl
from jax.experimental.pallas import tpu as pltpu

EPS = 1e-6
N_MOD = 6
VMEM_LIMIT_BYTES = 56 * 1024 * 1024
SUBLANES = 8
LANES = 128
HALO = 32
TILE_T = 512
CONV_ROWS = 32


def _bf16(a):
    return a.astype(jnp.bfloat16)


def _dot(a, b):
    return jnp.dot(a, b, preferred_element_type=jnp.float32)


def _rms_mod(x, g, sc, sh):
    ms = jnp.mean(x * x, axis=-1, keepdims=True)
    return (x * lax.rsqrt(ms + EPS)) * (g * (1.0 + sc)) + sh


def _layernorm(v, g, b):
    mu = jnp.mean(v, axis=-1, keepdims=True)
    vc = v - mu
    var = jnp.mean(vc * vc, axis=-1, keepdims=True)
    return vc * lax.rsqrt(var + EPS) * g + b


def _ada_kernel(c_ref, w_ref, b_ref, o_ref):
    c = c_ref[...]
    c_act = _bf16(c * jax.nn.sigmoid(c))
    o_ref[0] = _dot(c_act, _bf16(w_ref[0])) + b_ref[0]


def _ada_call(c, w_ada, b_ada):
    depth, d, ncols = w_ada.shape
    bsz = c.shape[0]
    col_tile = ncols // 3
    return pl.pallas_call(
        _ada_kernel,
        grid=(depth, ncols // col_tile),
        in_specs=[
            pl.BlockSpec((bsz, d), lambda l, j: (0, 0)),
            pl.BlockSpec((1, d, col_tile), lambda l, j: (l, 0, j)),
            pl.BlockSpec((1, 1, col_tile), lambda l, j: (l, 0, j)),
        ],
        out_specs=pl.BlockSpec((1, bsz, col_tile), lambda l, j: (l, 0, j)),
        out_shape=jax.ShapeDtypeStruct((depth, bsz, ncols), jnp.float32),
        compiler_params=pltpu.CompilerParams(
            dimension_semantics=("arbitrary", "arbitrary"),
            vmem_limit_bytes=VMEM_LIMIT_BYTES),
        name="adaln_mod",
    )(c, w_ada, b_ada.reshape(depth, 1, ncols))


def _mixer_kernel(x_ref, mod_ref, g1_ref, w_in_ref, alng_ref, alnb_ref, ws_ref, bs_ref,
                  w_pa_ref, cw_ref, cb_ref, blng_ref, blnb_ref, w_pb_ref, w_out_ref,
                  o_ref, zs_ref, cv_ref, *, heads, chunk, conv_width):
    t = pl.program_id(1)
    tile_t, d = x_ref.shape[1], x_ref.shape[2]
    aw = alng_ref.shape[1]
    bw = blng_ref.shape[1]
    hd = aw // heads

    x = x_ref[0]
    mod = mod_ref[0]
    hb = _bf16(_rms_mod(x, g1_ref[...], mod[1:2], mod[0:1]))

    u = _dot(hb, w_in_ref[:, 0:aw])
    v = _dot(hb, w_in_ref[:, aw:2 * aw])
    vb = _bf16(_layernorm(v, alng_ref[...], alnb_ref[...]))
    row = lax.broadcasted_iota(jnp.int32, (chunk, chunk), 0)
    col = lax.broadcasted_iota(jnp.int32, (chunk, chunk), 1)
    s_rows = []
    for ci in range(tile_t // chunk):
        s_heads = []
        for h in range(heads):
            w_h = _bf16(jnp.where(col <= row, ws_ref[h], 0.0))
            s_heads.append(_dot(w_h, vb[ci * chunk:(ci + 1) * chunk, h * hd:(h + 1) * hd]))
        s_rows.append(jnp.concatenate(s_heads, axis=1) + bs_ref[...])
    s = jnp.concatenate(s_rows, axis=0)
    y_a = _dot(_bf16(u * s), w_pa_ref[...])

    off = 2 * aw
    pa = _dot(hb, w_in_ref[:, off:off + bw])
    pg = _dot(hb, w_in_ref[:, off + bw:off + 2 * bw])

    @pl.when(t == 0)
    def _():
        zs_ref[0:HALO, :] = jnp.zeros((HALO, bw), jnp.float32)

    zs_ref[HALO:HALO + tile_t, :] = pa * jax.nn.sigmoid(pg)

    first = HALO - (conv_width - 1)
    span = CONV_ROWS + HALO

    def conv_block(i, carry):
        base = pl.multiple_of(i * CONV_ROWS, CONV_ROWS)
        for c0 in range(0, bw, LANES):
            zz = zs_ref[pl.ds(base, span), c0:c0 + LANES]
            acc = jnp.zeros((CONV_ROWS, LANES), jnp.float32) + cb_ref[:, c0:c0 + LANES]
            for k in range(conv_width):
                acc = acc + cw_ref[k:k + 1, c0:c0 + LANES] * zz[first + k:first + k + CONV_ROWS, :]
            cv_ref[pl.ds(base, CONV_ROWS), c0:c0 + LANES] = acc
        return carry

    lax.fori_loop(0, tile_t // CONV_ROWS, conv_block, 0)
    zs_ref[0:HALO, :] = zs_ref[tile_t:tile_t + HALO, :]

    zc = _layernorm(cv_ref[0:tile_t, :], blng_ref[...], blnb_ref[...])
    y_b = _dot(_bf16(zc * jax.nn.sigmoid(zc)), w_pb_ref[...])

    off = 2 * aw + 2 * bw
    g_a = _dot(hb, w_in_ref[:, off:off + d])
    g_b = _dot(hb, w_in_ref[:, off + d:off + 2 * d])
    merged = jax.nn.sigmoid(g_a) * y_a + jax.nn.sigmoid(g_b) * y_b
    o_ref[0] = x + mod[2:3] * _dot(_bf16(merged), w_out_ref[...])


def _const_spec(shape):
    nd = len(shape)
    return pl.BlockSpec(shape, lambda b, t: (0,) * nd, pipeline_mode=pl.Buffered(1))


def _mixer_call(x, mod, g1, w_in, alng, alnb, ws, bs_full, w_pa, cw, cb, blng, blnb, w_pb, w_out):
    bsz, seq, d = x.shape
    heads, chunk = ws.shape[0], ws.shape[1]
    conv_width, bw = cw.shape
    assert seq % TILE_T == 0 and TILE_T % chunk == 0 and TILE_T % CONV_ROWS == 0
    assert conv_width - 1 <= HALO and bw % LANES == 0
    kern = functools.partial(_mixer_kernel, heads=heads, chunk=chunk, conv_width=conv_width)
    consts = (g1, w_in, alng, alnb, ws, bs_full, w_pa, cw, cb, blng, blnb, w_pb, w_out)
    return pl.pallas_call(
        kern,
        grid=(bsz, seq // TILE_T),
        in_specs=[
            pl.BlockSpec((1, TILE_T, d), lambda b, t: (b, t, 0)),
            pl.BlockSpec((1, N_MOD, d), lambda b, t: (b, 0, 0)),
        ] + [_const_spec(a.shape) for a in consts],
        out_specs=pl.BlockSpec((1, TILE_T, d), lambda b, t: (b, t, 0)),
        out_shape=jax.ShapeDtypeStruct(x.shape, jnp.float32),
        scratch_shapes=[
            pltpu.VMEM((HALO + TILE_T, bw), jnp.float32),
            pltpu.VMEM((TILE_T, bw), jnp.float32),
        ],
        compiler_params=pltpu.CompilerParams(
            dimension_semantics=("arbitrary", "arbitrary"),
            vmem_limit_bytes=VMEM_LIMIT_BYTES),
        name="mixer",
    )(x, mod, *consts)


def _ffn_kernel(x_ref, mod_ref, g2_ref, w1_ref, w2_ref, gf_ref, o_ref, *, hidden_tile, final_norm):
    x = x_ref[0]
    mod = mod_ref[0]
    hb = _bf16(_rms_mod(x, g2_ref[...], mod[4:5], mod[3:4]))
    hidden = w1_ref.shape[1]
    acc = jnp.zeros(x.shape, jnp.float32)
    for j in range(0, hidden, hidden_tile):
        a = jnp.maximum(_dot(hb, w1_ref[:, j:j + hidden_tile]), 0.0)
        acc = acc + _dot(_bf16(a * a), w2_ref[j:j + hidden_tile, :])
    y = x + mod[5:6] * acc
    if final_norm:
        ms = jnp.mean(y * y, axis=-1, keepdims=True)
        y = (y * lax.rsqrt(ms + EPS)) * gf_ref[...]
    o_ref[0] = y


def _ffn_call(x, mod, g2, w1, w2, gf, final_norm):
    bsz, seq, d = x.shape
    hidden = w1.shape[1]
    kern = functools.partial(_ffn_kernel, hidden_tile=min(hidden, 1024), final_norm=final_norm)
    consts = (g2, w1, w2, gf)
    return pl.pallas_call(
        kern,
        grid=(bsz, seq // TILE_T),
        in_specs=[
            pl.BlockSpec((1, TILE_T, d), lambda b, t: (b, t, 0)),
            pl.BlockSpec((1, N_MOD, d), lambda b, t: (b, 0, 0)),
        ] + [_const_spec(a.shape) for a in consts],
        out_specs=pl.BlockSpec((1, TILE_T, d), lambda b, t: (b, t, 0)),
        out_shape=jax.ShapeDtypeStruct(x.shape, jnp.float32),
        compiler_params=pltpu.CompilerParams(
            dimension_semantics=("arbitrary", "arbitrary"),
            vmem_limit_bytes=VMEM_LIMIT_BYTES),
        name="ffn",
    )(x, mod, *consts)


def kernel(x, c, w_ada, b_ada, norm1_g, w_in, a_ln_g, a_ln_b, a_ws, a_bs, w_pa,
           b_conv_w, b_conv_b, b_ln_g, b_ln_b, w_pb, w_out, norm2_g, w_ff1, w_ff2, final_g):
    depth, d = norm1_g.shape
    bsz = x.shape[0]
    heads, chunk = a_ws.shape[1], a_ws.shape[2]
    aw = a_ln_g.shape[1]
    hd = aw // heads

    mod = _ada_call(c, w_ada, b_ada).reshape(depth, bsz, N_MOD, d)
    bs_full = jnp.repeat(jnp.swapaxes(a_bs, 1, 2), hd, axis=2)
    row = lambda a, l: a[l].reshape(1, -1)

    for l in range(depth):
        x = _mixer_call(
            x, mod[l], row(norm1_g, l), _bf16(w_in[l]), row(a_ln_g, l), row(a_ln_b, l),
            a_ws[l], bs_full[l], _bf16(w_pa[l]), b_conv_w[l, :, 0, :], row(b_conv_b, l),
            row(b_ln_g, l), row(b_ln_b, l), _bf16(w_pb[l]), _bf16(w_out[l]))
        x = _ffn_call(x, mod[l], row(norm2_g, l), _bf16(w_ff1[l]), _bf16(w_ff2[l]),
                      final_g.reshape(1, -1), final_norm=(l == depth - 1))
    return x
```

```python
import functools

import jax
import jax.numpy as jnp
from jax import lax
from jax.experimental import pallas as pl
from jax.experimental.pallas import tpu as pltpu

EPS = 1e-6
N_MOD = 6
VMEM_LIMIT_BYTES = 56 * 1024 * 1024
SUBLANES = 8
LANES = 128
HALO = 32
TILE_T = 512
CONV_ROWS = 64


def _bf16(a):
    return a.astype(jnp.bfloat16)


def _dot(a, b):
    return jnp.dot(a, b, preferred_element_type=jnp.float32)


def _rms_mod(x, g, sc, sh):
    ms = jnp.mean(x * x, axis=-1, keepdims=True)
    return (x * lax.rsqrt(ms + EPS)) * (g * (1.0 + sc)) + sh


def _layernorm(v, g, b):
    mu = jnp.mean(v, axis=-1, keepdims=True)
    vc = v - mu
    var = jnp.mean(vc * vc, axis=-1, keepdims=True)
    return vc * lax.rsqrt(var + EPS) * g + b


def _ada_kernel(c_ref, w_ref, b_ref, o_ref):
    c = c_ref[...]
    c_act = _bf16(c * jax.nn.sigmoid(c))
    o_ref[0] = _dot(c_act, _bf16(w_ref[0])) + b_ref[0]


def _ada_call(c, w_ada, b_ada):
    depth, d, ncols = w_ada.shape
    bsz = c.shape[0]
    col_tile = ncols // 3
    return pl.pallas_call(
        _ada_kernel,
        grid=(depth, ncols // col_tile),
        in_specs=[
            pl.BlockSpec((bsz, d), lambda l, j: (0, 0)),
            pl.BlockSpec((1, d, col_tile), lambda l, j: (l, 0, j)),
            pl.BlockSpec((1, 1, col_tile), lambda l, j: (l, 0, j)),
        ],
        out_specs=pl.BlockSpec((1, bsz, col_tile), lambda l, j: (l, 0, j)),
        out_shape=jax.ShapeDtypeStruct((depth, bsz, ncols), jnp.float32),
        compiler_params=pltpu.CompilerParams(
            dimension_semantics=("arbitrary", "arbitrary"),
            vmem_limit_bytes=VMEM_LIMIT_BYTES),
        name="adaln_mod",
    )(c, w_ada, b_ada.reshape(depth, 1, ncols))


def _mixer_kernel(x_ref, mod_ref, g1_ref, w_in_ref, alng_ref, alnb_ref, ws_ref, bs_ref,
                  w_pa_ref, cw_ref, cb_ref, blng_ref, blnb_ref, w_pb_ref, w_out_ref,
                  o_ref, zs_ref, cv_ref, *, heads, chunk, conv_width):
    t = pl.program_id(1)
    tile_t, d = x_ref.shape[1], x_ref.shape[2]
    aw = alng_ref.shape[1]
    bw = blng_ref.shape[1]
    hd = aw // heads

    x = x_ref[0]
    mod = mod_ref[0]
    hb = _bf16(_rms_mod(x, g1_ref[...], mod[1:2], mod[0:1]))

    u = _dot(hb, w_in_ref[:, 0:aw])
    v = _dot(hb, w_in_ref[:, aw:2 * aw])
    vb = _bf16(_layernorm(v, alng_ref[...], alnb_ref[...]))
    row = lax.broadcasted_iota(jnp.int32, (chunk, chunk), 0)
    col = lax.broadcasted_iota(jnp.int32, (chunk, chunk), 1)
    s_rows = []
    for ci in range(tile_t // chunk):
        s_heads = []
        for h in range(heads):
            w_h = _bf16(jnp.where(col <= row, ws_ref[h], 0.0))
            s_heads.append(_dot(w_h, vb[ci * chunk:(ci + 1) * chunk, h * hd:(h + 1) * hd]))
        s_rows.append(jnp.concatenate(s_heads, axis=1) + bs_ref[...])
    s = jnp.concatenate(s_rows, axis=0)
    y_a = _dot(_bf16(u * s), w_pa_ref[...])

    off = 2 * aw
    pa = _dot(hb, w_in_ref[:, off:off + bw])
    pg = _dot(hb, w_in_ref[:, off + bw:off + 2 * bw])

    @pl.when(t == 0)
    def _():
        zs_ref[0:HALO, :] = jnp.zeros((HALO, bw), jnp.float32)

    zs_ref[HALO:HALO + tile_t, :] = pa * jax.nn.sigmoid(pg)

    first = HALO - (conv_width - 1)
    span = CONV_ROWS + HALO

    def conv_block(i, carry):
        base = pl.multiple_of(i * CONV_ROWS, CONV_ROWS)
        for c0 in range(0, bw, LANES):
            zz = zs_ref[pl.ds(base, span), c0:c0 + LANES]
            acc = jnp.zeros((CONV_ROWS, LANES), jnp.float32) + cb_ref[:, c0:c0 + LANES]
            for r in range(SUBLANES):
                rows = CONV_ROWS if r == 0 else CONV_ROWS + SUBLANES
                p = None
                for q in range(span // SUBLANES):
                    k = SUBLANES * q + r - first
                    if 0 <= k < conv_width:
                        term = (cw_ref[k:k + 1, c0:c0 + LANES]
                                * zz[SUBLANES * q:SUBLANES * q + rows, :])
                        p = term if p is None else p + term
                if p is not None:
                    acc = acc + p[r:r + CONV_ROWS]
            cv_ref[pl.ds(base, CONV_ROWS), c0:c0 + LANES] = acc
        return carry

    lax.fori_loop(0, tile_t // CONV_ROWS, conv_block, 0)
    zs_ref[0:HALO, :] = zs_ref[tile_t:tile_t + HALO, :]

    zc = _layernorm(cv_ref[0:tile_t, :], blng_ref[...], blnb_ref[...])
    y_b = _dot(_bf16(zc * jax.nn.sigmoid(zc)), w_pb_ref[...])

    off = 2 * aw + 2 * bw
    g_a = _dot(hb, w_in_ref[:, off:off + d])
    g_b = _dot(hb, w_in_ref[:, off + d:off + 2 * d])
    merged = jax.nn.sigmoid(g_a) * y_a + jax.nn.sigmoid(g_b) * y_b
    o_ref[0] = x + mod[2:3] * _dot(_bf16(merged), w_out_ref[...])


def _const_spec(shape):
    nd = len(shape)
    return pl.BlockSpec(shape, lambda b, t: (0,) * nd, pipeline_mode=pl.Buffered(1))


def _mixer_call(x, mod, g1, w_in, alng, alnb, ws, bs_full, w_pa, cw, cb, blng, blnb, w_pb, w_out):
    bsz, seq, d = x.shape
    heads, chunk = ws.shape[0], ws.shape[1]
    conv_width, bw = cw.shape
    assert seq % TILE_T == 0 and TILE_T % chunk == 0 and TILE_T % CONV_ROWS == 0
    assert conv_width - 1 <= HALO and bw % LANES == 0
    kern = functools.partial(_mixer_kernel, heads=heads, chunk=chunk, conv_width=conv_width)
    consts = (g1, w_in, alng, alnb, ws, bs_full, w_pa, cw, cb, blng, blnb, w_pb, w_out)
    return pl.pallas_call(
        kern,
        grid=(bsz, seq // TILE_T),
        in_specs=[
            pl.BlockSpec((1, TILE_T, d), lambda b, t: (b, t, 0)),
            pl.BlockSpec((1, N_MOD, d), lambda b, t: (b, 0, 0)),
        ] + [_const_spec(a.shape) for a in consts],
        out_specs=pl.BlockSpec((1, TILE_T, d), lambda b, t: (b, t, 0)),
        out_shape=jax.ShapeDtypeStruct(x.shape, jnp.float32),
        scratch_shapes=[
            pltpu.VMEM((HALO + TILE_T, bw), jnp.float32),
            pltpu.VMEM((TILE_T, bw), jnp.float32),
        ],
        compiler_params=pltpu.CompilerParams(
            dimension_semantics=("arbitrary", "arbitrary"),
            vmem_limit_bytes=VMEM_LIMIT_BYTES),
        name="mixer",
    )(x, mod, *consts)


def _ffn_kernel(x_ref, mod_ref, g2_ref, w1_ref, w2_ref, gf_ref, o_ref, *, hidden_tile, final_norm):
    x = x_ref[0]
    mod = mod_ref[0]
    hb = _bf16(_rms_mod(x, g2_ref[...], mod[4:5], mod[3:4]))
    hidden = w1_ref.shape[1]
    acc = jnp.zeros(x.shape, jnp.float32)
    for j in range(0, hidden, hidden_tile):
        a = jnp.maximum(_dot(hb, w1_ref[:, j:j + hidden_tile]), 0.0)
        acc = acc + _dot(_bf16(a * a), w2_ref[j:j + hidden_tile, :])
    y = x + mod[5:6] * acc
    if final_norm:
        ms = jnp.mean(y * y, axis=-1, keepdims=True)
        y = (y * lax.rsqrt(ms + EPS)) * gf_ref[...]
    o_ref[0] = y


def _ffn_call(x, mod, g2, w1, w2, gf, final_norm):
    bsz, seq, d = x.shape
    hidden = w1.shape[1]
    kern = functools.partial(_ffn_kernel, hidden_tile=min(hidden, 1024), final_norm=final_norm)
    consts = (g2, w1, w2, gf)
    return pl.pallas_call(
        kern,
        grid=(bsz, seq // TILE_T),
        in_specs=[
            pl.BlockSpec((1, TILE_T, d), lambda b, t: (b, t, 0)),
            pl.BlockSpec((1, N_MOD, d), lambda b, t: (b, 0, 0)),
        ] + [_const_spec(a.shape) for a in consts],
        out_specs=pl.BlockSpec((1, TILE_T, d), lambda b, t: (b, t, 0)),
        out_shape=jax.ShapeDtypeStruct(x.shape, jnp.float32),
        compiler_params=pltpu.CompilerParams(
            dimension_semantics=("arbitrary", "arbitrary"),
            vmem_limit_bytes=VMEM_LIMIT_BYTES),
        name="ffn",
    )(x, mod, *consts)


def kernel(x, c, w_ada, b_ada, norm1_g, w_in, a_ln_g, a_ln_b, a_ws, a_bs, w_pa,
           b_conv_w, b_conv_b, b_ln_g, b_ln_b, w_pb, w_out, norm2_g, w_ff1, w_ff2, final_g):
    depth, d = norm1_g.shape
    bsz = x.shape[0]
    heads, chunk = a_ws.shape[1], a_ws.shape[2]
    aw = a_ln_g.shape[1]
    hd = aw // heads

    mod = _ada_call(c, w_ada, b_ada).reshape(depth, bsz, N_MOD, d)
    bs_full = jnp.repeat(jnp.swapaxes(a_bs, 1, 2), hd, axis=2)
    row = lambda a, l: a[l].reshape(1, -1)

    for l in range(depth):
        x = _mixer_call(
            x, mod[l], row(norm1_g, l), _bf16(w_in[l]), row(a_ln_g, l), row(a_ln_b, l),
            a_ws[l], bs_full[l], _bf16(w_pa[l]), b_conv_w[l, :, 0, :], row(b_conv_b, l),
            row(b_ln_g, l), row(b_ln_b, l), _bf16(w_pb[l]), _bf16(w_out[l]))
        x = _ffn_call(x, mod[l], row(norm2_g, l), _bf16(w_ff1[l]), _bf16(w_ff2[l]),
                      final_g.reshape(1, -1), final_norm=(l == depth - 1))
    return x
```

```python
import functools

import jax
import jax.numpy as jnp
from jax import lax
from jax.experimental import pallas as pl
from jax.experimental.pallas import tpu as pltpu

EPS = 1e-6
N_MOD = 6
VMEM_LIMIT_BYTES = 56 * 1024 * 1024
SUBLANES = 8
LANES = 128
HALO = 32
TILE_T = 512
CONV_ROWS = 64


def _bf16(a):
    return a.astype(jnp.bfloat16)


def _dot(a, b):
    return jnp.dot(a, b, preferred_element_type=jnp.float32)


def _rms_mod(x, g, sc, sh):
    ms = jnp.mean(x * x, axis=-1, keepdims=True)
    return (x * lax.rsqrt(ms + EPS)) * (g * (1.0 + sc)) + sh


def _row_sum(p):
    hi = _bf16(p)
    lo = _bf16(p - hi.astype(jnp.float32))
    ones = jnp.ones((2 * LANES, LANES), jnp.bfloat16)
    return _dot(jnp.concatenate([hi, lo], axis=1), ones)


def _layernorm(v, g, b):
    n = v.shape[1] // LANES
    cols = [v[:, j * LANES:(j + 1) * LANES] for j in range(n)]
    inv = 1.0 / v.shape[1]
    mu = _row_sum(functools.reduce(lambda a, c: a + c, cols)) * inv
    cen = [c - mu for c in cols]
    var = _row_sum(functools.reduce(lambda a, c: a + c, [c * c for c in cen])) * inv
    rstd = lax.rsqrt(var + EPS)
    out = [cen[j] * rstd * g[:, j * LANES:(j + 1) * LANES] + b[:, j * LANES:(j + 1) * LANES]
           for j in range(n)]
    return jnp.concatenate(out, axis=1)


def _ada_kernel(c_ref, w_ref, b_ref, o_ref):
    c = c_ref[...]
    c_act = _bf16(c * jax.nn.sigmoid(c))
    o_ref[0] = _dot(c_act, _bf16(w_ref[0])) + b_ref[0]


def _ada_call(c, w_ada, b_ada):
    depth, d, ncols = w_ada.shape
    bsz = c.shape[0]
    col_tile = ncols // 3
    return pl.pallas_call(
        _ada_kernel,
        grid=(depth, ncols // col_tile),
        in_specs=[
            pl.BlockSpec((bsz, d), lambda l, j: (0, 0)),
            pl.BlockSpec((1, d, col_tile), lambda l, j: (l, 0, j)),
            pl.BlockSpec((1, 1, col_tile), lambda l, j: (l, 0, j)),
        ],
        out_specs=pl.BlockSpec((1, bsz, col_tile), lambda l, j: (l, 0, j)),
        out_shape=jax.ShapeDtypeStruct((depth, bsz, ncols), jnp.float32),
        compiler_params=pltpu.CompilerParams(
            dimension_semantics=("arbitrary", "arbitrary"),
            vmem_limit_bytes=VMEM_LIMIT_BYTES),
        name="adaln_mod",
    )(c, w_ada, b_ada.reshape(depth, 1, ncols))


def _mixer_kernel(x_ref, mod_ref, g1_ref, w_in_ref, alng_ref, alnb_ref, ws_ref, bs_ref,
                  w_pa_ref, cw_ref, cb_ref, blng_ref, blnb_ref, w_pb_ref, w_out_ref,
                  o_ref, zs_ref, cv_ref, pr_ref, hb_ref, *, heads, chunk, conv_width):
    t = pl.program_id(1)
    tile_t, d = x_ref.shape[1], x_ref.shape[2]
    aw = alng_ref.shape[1]
    bw = blng_ref.shape[1]
    hd = aw // heads

    @pl.when(t == 0)
    def _():
        zs_ref[0:HALO, :] = jnp.zeros((HALO, bw), jnp.float32)

    mod = mod_ref[0]
    hb_ref[...] = _bf16(_rms_mod(x_ref[0], g1_ref[...], mod[1:2], mod[0:1]))
    hb = hb_ref[...]

    pa = _dot(hb, w_in_ref[:, 0:bw])
    pg = _dot(hb, w_in_ref[:, bw:2 * bw])
    zs_ref[HALO:HALO + tile_t, :] = pa * jax.nn.sigmoid(pg)

    pr_ref[...] = _dot(hb, w_in_ref[:, 2 * bw:])

    first = HALO - (conv_width - 1)
    span = CONV_ROWS + HALO

    for c0 in range(0, bw, LANES):
        for i in range(tile_t // CONV_ROWS):
            base = i * CONV_ROWS
            zz = zs_ref[base:base + span, c0:c0 + LANES]
            acc = jnp.zeros((CONV_ROWS, LANES), jnp.float32) + cb_ref[:, c0:c0 + LANES]
            for r in range(SUBLANES):
                rows = CONV_ROWS if r == 0 else CONV_ROWS + SUBLANES
                p = None
                for q in range(span // SUBLANES):
                    k = SUBLANES * q + r - first
                    if 0 <= k < conv_width:
                        term = (cw_ref[k:k + 1, c0:c0 + LANES]
                                * zz[SUBLANES * q:SUBLANES * q + rows, :])
                        p = term if p is None else p + term
                if p is not None:
                    acc = acc + p[r:r + CONV_ROWS]
            cv_ref[base:base + CONV_ROWS, c0:c0 + LANES] = acc
    zs_ref[0:HALO, :] = zs_ref[tile_t:tile_t + HALO, :]

    zc = _layernorm(cv_ref[...], blng_ref[...], blnb_ref[...])
    y_b = _dot(_bf16(zc * jax.nn.sigmoid(zc)), w_pb_ref[...])

    vb = _bf16(_layernorm(pr_ref[:, aw:2 * aw], alng_ref[...], alnb_ref[...]))
    row = lax.broadcasted_iota(jnp.int32, (chunk, chunk), 0)
    col = lax.broadcasted_iota(jnp.int32, (chunk, chunk), 1)
    s_rows = []
    for ci in range(tile_t // chunk):
        s_heads = []
        for h in range(heads):
            w_h = _bf16(jnp.where(col <= row, ws_ref[h], 0.0))
            s_heads.append(_dot(w_h, vb[ci * chunk:(ci + 1) * chunk, h * hd:(h + 1) * hd]))
        s_rows.append(jnp.concatenate(s_heads, axis=1) + bs_ref[...])
    s = jnp.concatenate(s_rows, axis=0)
    y_a = _dot(_bf16(pr_ref[:, 0:aw] * s), w_pa_ref[...])

    g_a = pr_ref[:, 2 * aw:2 * aw + d]
    g_b = pr_ref[:, 2 * aw + d:2 * aw + 2 * d]
    merged = jax.nn.sigmoid(g_a) * y_a + jax.nn.sigmoid(g_b) * y_b
    o_ref[0] = x_ref[0] + mod[2:3] * _dot(_bf16(merged), w_out_ref[...])


def _const_spec(shape):
    nd = len(shape)
    return pl.BlockSpec(shape, lambda b, t: (0,) * nd, pipeline_mode=pl.Buffered(1))


def _mixer_call(x, mod, g1, w_in, alng, alnb, ws, bs_full, w_pa, cw, cb, blng, blnb, w_pb, w_out):
    bsz, seq, d = x.shape
    heads, chunk = ws.shape[0], ws.shape[1]
    conv_width, bw = cw.shape
    aw = alng.shape[1]
    assert seq % TILE_T == 0 and TILE_T % chunk == 0 and TILE_T % CONV_ROWS == 0
    assert conv_width - 1 <= HALO and bw % LANES == 0 and aw % LANES == 0
    kern = functools.partial(_mixer_kernel, heads=heads, chunk=chunk, conv_width=conv_width)
    consts = (g1, w_in, alng, alnb, ws, bs_full, w_pa, cw, cb, blng, blnb, w_pb, w_out)
    return pl.pallas_call(
        kern,
        grid=(bsz, seq // TILE_T),
        in_specs=[
            pl.BlockSpec((1, TILE_T, d), lambda b, t: (b, t, 0)),
            pl.BlockSpec((1, N_MOD, d), lambda b, t: (b, 0, 0)),
        ] + [_const_spec(a.shape) for a in consts],
        out_specs=pl.BlockSpec((1, TILE_T, d), lambda b, t: (b, t, 0)),
        out_shape=jax.ShapeDtypeStruct(x.shape, jnp.float32),
        scratch_shapes=[
            pltpu.VMEM((HALO + TILE_T, bw), jnp.float32),
            pltpu.VMEM((TILE_T, bw), jnp.float32),
            pltpu.VMEM((TILE_T, 2 * aw + 2 * d), jnp.float32),
            pltpu.VMEM((TILE_T, d), jnp.bfloat16),
        ],
        compiler_params=pltpu.CompilerParams(
            dimension_semantics=("arbitrary", "arbitrary"),
            vmem_limit_bytes=VMEM_LIMIT_BYTES),
        name="mixer",
    )(x, mod, *consts)


def _ffn_kernel(x_ref, mod_ref, g2_ref, w1_ref, w2_ref, gf_ref, o_ref, *, hidden_tile, final_norm):
    x = x_ref[0]
    mod = mod_ref[0]
    hb = _bf16(_rms_mod(x, g2_ref[...], mod[4:5], mod[3:4]))
    hidden = w1_ref.shape[1]
    acc = jnp.zeros(x.shape, jnp.float32)
    for j in range(0, hidden, hidden_tile):
        a = jnp.maximum(_dot(hb, w1_ref[:, j:j + hidden_tile]), 0.0)
        acc = acc + _dot(_bf16(a * a), w2_ref[j:j + hidden_tile, :])
    y = x + mod[5:6] * acc
    if final_norm:
        ms = jnp.mean(y * y, axis=-1, keepdims=True)
        y = (y * lax.rsqrt(ms + EPS)) * gf_ref[...]
    o_ref[0] = y


def _ffn_call(x, mod, g2, w1, w2, gf, final_norm):
    bsz, seq, d = x.shape
    hidden = w1.shape[1]
    kern = functools.partial(_ffn_kernel, hidden_tile=min(hidden, 1024), final_norm=final_norm)
    consts = (g2, w1, w2, gf)
    return pl.pallas_call(
        kern,
        grid=(bsz, seq // TILE_T),
        in_specs=[
            pl.BlockSpec((1, TILE_T, d), lambda b, t: (b, t, 0)),
            pl.BlockSpec((1, N_MOD, d), lambda b, t: (b, 0, 0)),
        ] + [_const_spec(a.shape) for a in consts],
        out_specs=pl.BlockSpec((1, TILE_T, d), lambda b, t: (b, t, 0)),
        out_shape=jax.ShapeDtypeStruct(x.shape, jnp.float32),
        compiler_params=pltpu.CompilerParams(
            dimension_semantics=("arbitrary", "arbitrary"),
            vmem_limit_bytes=VMEM_LIMIT_BYTES),
        name="ffn",
    )(x, mod, *consts)


def kernel(x, c, w_ada, b_ada, norm1_g, w_in, a_ln_g, a_ln_b, a_ws, a_bs, w_pa,
           b_conv_w, b_conv_b, b_ln_g, b_ln_b, w_pb, w_out, norm2_g, w_ff1, w_ff2, final_g):
    depth, d = norm1_g.shape
    bsz = x.shape[0]
    heads = a_ws.shape[1]
    aw = a_ln_g.shape[1]
    bw = b_ln_g.shape[1]
    hd = aw // heads

    mod = _ada_call(c, w_ada, b_ada).reshape(depth, bsz, N_MOD, d)
    bs_full = jnp.repeat(jnp.swapaxes(a_bs, 1, 2), hd, axis=2)
    w_in_r = _bf16(jnp.concatenate(
        [w_in[:, :, 2 * aw:2 * aw + 2 * bw], w_in[:, :, :2 * aw], w_in[:, :, 2 * aw + 2 * bw:]],
        axis=2))
    row = lambda a, l: a[l].reshape(1, -1)

    for l in range(depth):
        x = _mixer_call(
            x, mod[l], row(norm1_g, l), w_in_r[l], row(a_ln_g, l), row(a_ln_b, l),
            a_ws[l], bs_full[l], _bf16(w_pa[l]), b_conv_w[l, :, 0, :], row(b_conv_b, l),
            row(b_ln_g, l), row(b_ln_b, l), _bf16(w_pb[l]), _bf16(w_out[l]))
        x = _ffn_call(x, mod[l], row(norm2_g, l), _bf16(w_ff1[l]), _bf16(w_ff2[l]),
                      final_g.reshape(1, -1), final_norm=(l == depth - 1))
    return x
```

```python
import functools

import jax
import jax.numpy as jnp
from jax import lax
from jax.experimental import pallas as pl
from jax.experimental.pallas import tpu as pltpu

EPS = 1e-6
N_MOD = 6
VMEM_LIMIT_BYTES = 56 * 1024 * 1024
SUBLANES = 8
LANES = 128
HALO = 32
TILE_T = 512
CONV_ROWS = 64


def _bf16(a):
    return a.astype(jnp.bfloat16)


def _dot(a, b):
    return jnp.dot(a, b, preferred_element_type=jnp.float32)


def _rms_mod(x, g, sc, sh):
    ms = jnp.mean(x * x, axis=-1, keepdims=True)
    return (x * lax.rsqrt(ms + EPS)) * (g * (1.0 + sc)) + sh


def _row_sum(p):
    hi = _bf16(p)
    lo = _bf16(p - hi.astype(jnp.float32))
    ones = jnp.ones((2 * LANES, LANES), jnp.bfloat16)
    return _dot(jnp.concatenate([hi, lo], axis=1), ones)


def _layernorm(v, g, b):
    n = v.shape[1] // LANES
    cols = [v[:, j * LANES:(j + 1) * LANES] for j in range(n)]
    inv = 1.0 / v.shape[1]
    mu = _row_sum(functools.reduce(lambda a, c: a + c, cols)) * inv
    cen = [c - mu for c in cols]
    var = _row_sum(functools.reduce(lambda a, c: a + c, [c * c for c in cen])) * inv
    rstd = lax.rsqrt(var + EPS)
    out = [cen[j] * rstd * g[:, j * LANES:(j + 1) * LANES] + b[:, j * LANES:(j + 1) * LANES]
           for j in range(n)]
    return jnp.concatenate(out, axis=1)


def _ada_kernel(c_ref, w_ref, b_ref, o_ref):
    c = c_ref[...]
    c_act = _bf16(c * jax.nn.sigmoid(c))
    o_ref[0] = _dot(c_act, _bf16(w_ref[0])) + b_ref[0]


def _ada_call(c, w_ada, b_ada):
    depth, d, ncols = w_ada.shape
    bsz = c.shape[0]
    col_tile = ncols // 3
    return pl.pallas_call(
        _ada_kernel,
        grid=(depth, ncols // col_tile),
        in_specs=[
            pl.BlockSpec((bsz, d), lambda l, j: (0, 0)),
            pl.BlockSpec((1, d, col_tile), lambda l, j: (l, 0, j)),
            pl.BlockSpec((1, 1, col_tile), lambda l, j: (l, 0, j)),
        ],
        out_specs=pl.BlockSpec((1, bsz, col_tile), lambda l, j: (l, 0, j)),
        out_shape=jax.ShapeDtypeStruct((depth, bsz, ncols), jnp.float32),
        compiler_params=pltpu.CompilerParams(
            dimension_semantics=("arbitrary", "arbitrary"),
            vmem_limit_bytes=VMEM_LIMIT_BYTES),
        name="adaln_mod",
    )(c, w_ada, b_ada.reshape(depth, 1, ncols))


def _mixer_kernel(x_ref, mod_ref, g1_ref, w_in_ref, alng_ref, alnb_ref, ws_ref, bs_ref,
                  w_pa_ref, cw_ref, cb_ref, blng_ref, blnb_ref, w_pb_ref, w_out_ref,
                  o_ref, zs_ref, cv_ref, pr_ref, hb_ref, *, heads, chunk, conv_width):
    t = pl.program_id(1)
    tile_t, d = x_ref.shape[1], x_ref.shape[2]
    aw = alng_ref.shape[1]
    bw = blng_ref.shape[1]
    hd = aw // heads

    @pl.when(t == 0)
    def _():
        zs_ref[0:HALO, :] = jnp.zeros((HALO, bw), jnp.float32)

    mod = mod_ref[0]
    hb_ref[...] = _bf16(_rms_mod(x_ref[0], g1_ref[...], mod[1:2], mod[0:1]))
    hb = hb_ref[...]

    for c0 in range(0, bw, LANES):
        pp = _dot(hb, w_in_ref[:, 2 * c0:2 * c0 + 2 * LANES])
        zs_ref[HALO:HALO + tile_t, c0:c0 + LANES] = pp[:, :LANES] * jax.nn.sigmoid(pp[:, LANES:])

    pr_ref[...] = _dot(hb, w_in_ref[:, 2 * bw:])

    first = HALO - (conv_width - 1)
    span = CONV_ROWS + HALO

    for c0 in range(0, bw, LANES):
        for i in range(tile_t // CONV_ROWS):
            base = i * CONV_ROWS
            zz = zs_ref[base:base + span, c0:c0 + LANES]
            acc = jnp.zeros((CONV_ROWS, LANES), jnp.float32) + cb_ref[:, c0:c0 + LANES]
            for r in range(SUBLANES):
                rows = CONV_ROWS if r == 0 else CONV_ROWS + SUBLANES
                p = None
                for q in range(span // SUBLANES):
                    k = SUBLANES * q + r - first
                    if 0 <= k < conv_width:
                        term = (cw_ref[k:k + 1, c0:c0 + LANES]
                                * zz[SUBLANES * q:SUBLANES * q + rows, :])
                        p = term if p is None else p + term
                if p is not None:
                    acc = acc + p[r:r + CONV_ROWS]
            cv_ref[base:base + CONV_ROWS, c0:c0 + LANES] = acc
    zs_ref[0:HALO, :] = zs_ref[tile_t:tile_t + HALO, :]

    zc = _layernorm(cv_ref[...], blng_ref[...], blnb_ref[...])
    y_b = _dot(_bf16(zc * jax.nn.sigmoid(zc)), w_pb_ref[...])

    vb = _bf16(_layernorm(pr_ref[:, aw:2 * aw], alng_ref[...], alnb_ref[...]))
    row = lax.broadcasted_iota(jnp.int32, (chunk, chunk), 0)
    col = lax.broadcasted_iota(jnp.int32, (chunk, chunk), 1)
    s_rows = []
    for ci in range(tile_t // chunk):
        s_heads = []
        for h in range(heads):
            w_h = _bf16(jnp.where(col <= row, ws_ref[h], 0.0))
            s_heads.append(_dot(w_h, vb[ci * chunk:(ci + 1) * chunk, h * hd:(h + 1) * hd]))
        s_rows.append(jnp.concatenate(s_heads, axis=1) + bs_ref[...])
    s = jnp.concatenate(s_rows, axis=0)
    y_a = _dot(_bf16(pr_ref[:, 0:aw] * s), w_pa_ref[...])

    g_a = pr_ref[:, 2 * aw:2 * aw + d]
    g_b = pr_ref[:, 2 * aw + d:2 * aw + 2 * d]
    merged = jax.nn.sigmoid(g_a) * y_a + jax.nn.sigmoid(g_b) * y_b
    o_ref[0] = x_ref[0] + mod[2:3] * _dot(_bf16(merged), w_out_ref[...])


def _const_spec(shape):
    nd = len(shape)
    return pl.BlockSpec(shape, lambda b, t: (0,) * nd, pipeline_mode=pl.Buffered(1))


def _mixer_call(x, mod, g1, w_in, alng, alnb, ws, bs_full, w_pa, cw, cb, blng, blnb, w_pb, w_out):
    bsz, seq, d = x.shape
    heads, chunk = ws.shape[0], ws.shape[1]
    conv_width, bw = cw.shape
    aw = alng.shape[1]
    assert seq % TILE_T == 0 and TILE_T % chunk == 0 and TILE_T % CONV_ROWS == 0
    assert conv_width - 1 <= HALO and bw % LANES == 0 and aw % LANES == 0
    kern = functools.partial(_mixer_kernel, heads=heads, chunk=chunk, conv_width=conv_width)
    consts = (g1, w_in, alng, alnb, ws, bs_full, w_pa, cw, cb, blng, blnb, w_pb, w_out)
    return pl.pallas_call(
        kern,
        grid=(bsz, seq // TILE_T),
        in_specs=[
            pl.BlockSpec((1, TILE_T, d), lambda b, t: (b, t, 0)),
            pl.BlockSpec((1, N_MOD, d), lambda b, t: (b, 0, 0)),
        ] + [_const_spec(a.shape) for a in consts],
        out_specs=pl.BlockSpec((1, TILE_T, d), lambda b, t: (b, t, 0)),
        out_shape=jax.ShapeDtypeStruct(x.shape, jnp.float32),
        scratch_shapes=[
            pltpu.VMEM((HALO + TILE_T, bw), jnp.float32),
            pltpu.VMEM((TILE_T, bw), jnp.float32),
            pltpu.VMEM((TILE_T, 2 * aw + 2 * d), jnp.float32),
            pltpu.VMEM((TILE_T, d), jnp.bfloat16),
        ],
        compiler_params=pltpu.CompilerParams(
            dimension_semantics=("arbitrary", "arbitrary"),
            vmem_limit_bytes=VMEM_LIMIT_BYTES),
        name="mixer",
    )(x, mod, *consts)


def _ffn_kernel(x_ref, mod_ref, g2_ref, w1_ref, w2_ref, gf_ref, o_ref, *, hidden_tile, final_norm):
    x = x_ref[0]
    mod = mod_ref[0]
    hb = _bf16(_rms_mod(x, g2_ref[...], mod[4:5], mod[3:4]))
    hidden = w1_ref.shape[1]
    acc = jnp.zeros(x.shape, jnp.float32)
    for j in range(0, hidden, hidden_tile):
        a = jnp.maximum(_dot(hb, w1_ref[:, j:j + hidden_tile]), 0.0)
        acc = acc + _dot(_bf16(a * a), w2_ref[j:j + hidden_tile, :])
    y = x + mod[5:6] * acc
    if final_norm:
        ms = jnp.mean(y * y, axis=-1, keepdims=True)
        y = (y * lax.rsqrt(ms + EPS)) * gf_ref[...]
    o_ref[0] = y


def _ffn_call(x, mod, g2, w1, w2, gf, final_norm):
    bsz, seq, d = x.shape
    hidden = w1.shape[1]
    kern = functools.partial(_ffn_kernel, hidden_tile=min(hidden, 1024), final_norm=final_norm)
    consts = (g2, w1, w2, gf)
    return pl.pallas_call(
        kern,
        grid=(bsz, seq // TILE_T),
        in_specs=[
            pl.BlockSpec((1, TILE_T, d), lambda b, t: (b, t, 0)),
            pl.BlockSpec((1, N_MOD, d), lambda b, t: (b, 0, 0)),
        ] + [_const_spec(a.shape) for a in consts],
        out_specs=pl.BlockSpec((1, TILE_T, d), lambda b, t: (b, t, 0)),
        out_shape=jax.ShapeDtypeStruct(x.shape, jnp.float32),
        compiler_params=pltpu.CompilerParams(
            dimension_semantics=("arbitrary", "arbitrary"),
            vmem_limit_bytes=VMEM_LIMIT_BYTES),
        name="ffn",
    )(x, mod, *consts)


def kernel(x, c, w_ada, b_ada, norm1_g, w_in, a_ln_g, a_ln_b, a_ws, a_bs, w_pa,
           b_conv_w, b_conv_b, b_ln_g, b_ln_b, w_pb, w_out, norm2_g, w_ff1, w_ff2, final_g):
    depth, d = norm1_g.shape
    bsz = x.shape[0]
    heads = a_ws.shape[1]
    aw = a_ln_g.shape[1]
    bw = b_ln_g.shape[1]
    hd = aw // heads

    mod = _ada_call(c, w_ada, b_ada).reshape(depth, bsz, N_MOD, d)
    bs_full = jnp.repeat(jnp.swapaxes(a_bs, 1, 2), hd, axis=2)
    conv_cols = w_in[:, :, 2 * aw:2 * aw + 2 * bw].reshape(depth, d, 2, bw // LANES, LANES)
    conv_cols = jnp.swapaxes(conv_cols, 2, 3).reshape(depth, d, 2 * bw)
    w_in_r = _bf16(jnp.concatenate(
        [conv_cols, w_in[:, :, :2 * aw], w_in[:, :, 2 * aw + 2 * bw:]], axis=2))
    row = lambda a, l: a[l].reshape(1, -1)

    for l in range(depth):
        x = _mixer_call(
            x, mod[l], row(norm1_g, l), w_in_r[l], row(a_ln_g, l), row(a_ln_b, l),
            a_ws[l], bs_full[l], _bf16(w_pa[l]), b_conv_w[l, :, 0, :], row(b_conv_b, l),
            row(b_ln_g, l), row(b_ln_b, l), _bf16(w_pb[l]), _bf16(w_out[l]))
        x = _ffn_call(x, mod[l], row(norm2_g, l), _bf16(w_ff1[l]), _bf16(w_ff2[l]),
                      final_g.reshape(1, -1), final_norm=(l == depth - 1))
    return x
```

```python
import functools

import jax
import jax.numpy as jnp
from jax import lax
from jax.experimental import pallas as pl
from jax.experimental.pallas import tpu as pltpu

EPS = 1e-6
N_MOD = 6
VMEM_LIMIT_BYTES = 56 * 1024 * 1024
SUBLANES = 8
LANES = 128
HALO = 32
TILE_T = 512
CONV_ROWS = 128


def _bf16(a):
    return a.astype(jnp.bfloat16)


def _dot(a, b):
    return jnp.dot(a, b, preferred_element_type=jnp.float32)


def _rms_mod(x, g, sc, sh):
    ms = jnp.mean(x * x, axis=-1, keepdims=True)
    return (x * lax.rsqrt(ms + EPS)) * (g * (1.0 + sc)) + sh


def _row_sum(p):
    hi = _bf16(p)
    lo = _bf16(p - hi.astype(jnp.float32))
    ones = jnp.ones((2 * LANES, LANES), jnp.bfloat16)
    return _dot(jnp.concatenate([hi, lo], axis=1), ones)


def _layernorm(v, g, b):
    n = v.shape[1] // LANES
    cols = [v[:, j * LANES:(j + 1) * LANES] for j in range(n)]
    inv = 1.0 / v.shape[1]
    mu = _row_sum(functools.reduce(lambda a, c: a + c, cols)) * inv
    cen = [c - mu for c in cols]
    var = _row_sum(functools.reduce(lambda a, c: a + c, [c * c for c in cen])) * inv
    rstd = lax.rsqrt(var + EPS)
    out = [cen[j] * rstd * g[:, j * LANES:(j + 1) * LANES] + b[:, j * LANES:(j + 1) * LANES]
           for j in range(n)]
    return jnp.concatenate(out, axis=1)


def _ada_kernel(c_ref, w_ref, b_ref, o_ref):
    c = c_ref[...]
    c_act = _bf16(c * jax.nn.sigmoid(c))
    o_ref[0] = _dot(c_act, _bf16(w_ref[0])) + b_ref[0]


def _ada_call(c, w_ada, b_ada):
    depth, d, ncols = w_ada.shape
    bsz = c.shape[0]
    col_tile = ncols // 3
    return pl.pallas_call(
        _ada_kernel,
        grid=(depth, ncols // col_tile),
        in_specs=[
            pl.BlockSpec((bsz, d), lambda l, j: (0, 0)),
            pl.BlockSpec((1, d, col_tile), lambda l, j: (l, 0, j)),
            pl.BlockSpec((1, 1, col_tile), lambda l, j: (l, 0, j)),
        ],
        out_specs=pl.BlockSpec((1, bsz, col_tile), lambda l, j: (l, 0, j)),
        out_shape=jax.ShapeDtypeStruct((depth, bsz, ncols), jnp.float32),
        compiler_params=pltpu.CompilerParams(
            dimension_semantics=("arbitrary", "arbitrary"),
            vmem_limit_bytes=VMEM_LIMIT_BYTES),
        name="adaln_mod",
    )(c, w_ada, b_ada.reshape(depth, 1, ncols))


def _mixer_kernel(x_ref, mod_ref, g1_ref, w_in_ref, alng_ref, alnb_ref, ws_ref, bs_ref,
                  w_pa_ref, cw_ref, cb_ref, blng_ref, blnb_ref, w_pb_ref, w_out_ref,
                  o_ref, zs_ref, cv_ref, pr_ref, hb_ref, *, heads, chunk, conv_width):
    t = pl.program_id(1)
    tile_t, d = x_ref.shape[1], x_ref.shape[2]
    aw = alng_ref.shape[1]
    bw = blng_ref.shape[1]
    hd = aw // heads

    @pl.when(t == 0)
    def _():
        zs_ref[0:HALO, :] = jnp.zeros((HALO, bw), jnp.float32)

    mod = mod_ref[...]
    hb_ref[...] = _bf16(_rms_mod(x_ref[0], g1_ref[...], mod[1:2], mod[0:1]))
    hb = hb_ref[...]

    pair = 2 * LANES
    for c0 in range(0, bw, pair):
        pa = _dot(hb, w_in_ref[:, c0:c0 + pair])
        pg = _dot(hb, w_in_ref[:, bw + c0:bw + c0 + pair])
        zs_ref[HALO:HALO + tile_t, c0:c0 + pair] = pa * jax.nn.sigmoid(pg)

    pr_ref[...] = _dot(hb, w_in_ref[:, 2 * bw:])

    first = HALO - (conv_width - 1)
    span = CONV_ROWS + HALO

    for c0 in range(0, bw, LANES):
        for i in range(tile_t // CONV_ROWS):
            base = i * CONV_ROWS
            zz = zs_ref[base:base + span, c0:c0 + LANES]
            acc = jnp.zeros((CONV_ROWS, LANES), jnp.float32) + cb_ref[:, c0:c0 + LANES]
            for r in range(SUBLANES):
                rows = CONV_ROWS if r == 0 else CONV_ROWS + SUBLANES
                p = None
                for q in range(span // SUBLANES):
                    k = SUBLANES * q + r - first
                    if 0 <= k < conv_width:
                        term = (cw_ref[k:k + 1, c0:c0 + LANES]
                                * zz[SUBLANES * q:SUBLANES * q + rows, :])
                        p = term if p is None else p + term
                if p is not None:
                    acc = acc + p[r:r + CONV_ROWS]
            cv_ref[base:base + CONV_ROWS, c0:c0 + LANES] = acc
    zs_ref[0:HALO, :] = zs_ref[tile_t:tile_t + HALO, :]

    zc = _layernorm(cv_ref[...], blng_ref[...], blnb_ref[...])
    y_b = _dot(_bf16(zc * jax.nn.sigmoid(zc)), w_pb_ref[...])

    vb = _bf16(_layernorm(pr_ref[:, aw:2 * aw], alng_ref[...], alnb_ref[...]))
    row = lax.broadcasted_iota(jnp.int32, (chunk, chunk), 0)
    col = lax.broadcasted_iota(jnp.int32, (chunk, chunk), 1)
    s_rows = []
    for ci in range(tile_t // chunk):
        s_heads = []
        for h in range(heads):
            w_h = _bf16(jnp.where(col <= row, ws_ref[h], 0.0))
            s_heads.append(_dot(w_h, vb[ci * chunk:(ci + 1) * chunk, h * hd:(h + 1) * hd]))
        s_rows.append(jnp.concatenate(s_heads, axis=1) + bs_ref[...])
    s = jnp.concatenate(s_rows, axis=0)
    y_a = _dot(_bf16(pr_ref[:, 0:aw] * s), w_pa_ref[...])

    g_a = pr_ref[:, 2 * aw:2 * aw + d]
    g_b = pr_ref[:, 2 * aw + d:2 * aw + 2 * d]
    merged = jax.nn.sigmoid(g_a) * y_a + jax.nn.sigmoid(g_b) * y_b
    o_ref[0] = x_ref[0] + mod[2:3] * _dot(_bf16(merged), w_out_ref[...])


def _layer_spec(shape, layer):
    nd = len(shape) - 1
    return pl.BlockSpec((None,) + tuple(shape[1:]), lambda b, t: (layer,) + (0,) * nd,
                        pipeline_mode=pl.Buffered(1))


def _mixer_call(layer, x, mod, g1, w_in, alng, alnb, ws, bs_full, w_pa, cw, cb, blng, blnb, w_pb, w_out):
    bsz, seq, d = x.shape
    heads, chunk = ws.shape[1], ws.shape[2]
    conv_width, bw = cw.shape[1], cw.shape[2]
    aw = alng.shape[2]
    assert seq % TILE_T == 0 and TILE_T % chunk == 0 and TILE_T % CONV_ROWS == 0
    assert conv_width - 1 <= HALO and bw % LANES == 0 and aw % LANES == 0
    kern = functools.partial(_mixer_kernel, heads=heads, chunk=chunk, conv_width=conv_width)
    consts = (g1, w_in, alng, alnb, ws, bs_full, w_pa, cw, cb, blng, blnb, w_pb, w_out)
    return pl.pallas_call(
        kern,
        grid=(bsz, seq // TILE_T),
        in_specs=[
            pl.BlockSpec((1, TILE_T, d), lambda b, t: (b, t, 0)),
            pl.BlockSpec((None, None, N_MOD, d), lambda b, t: (layer, b, 0, 0)),
        ] + [_layer_spec(a.shape, layer) for a in consts],
        out_specs=pl.BlockSpec((1, TILE_T, d), lambda b, t: (b, t, 0)),
        out_shape=jax.ShapeDtypeStruct(x.shape, jnp.float32),
        scratch_shapes=[
            pltpu.VMEM((HALO + TILE_T, bw), jnp.float32),
            pltpu.VMEM((TILE_T, bw), jnp.float32),
            pltpu.VMEM((TILE_T, 2 * aw + 2 * d), jnp.float32),
            pltpu.VMEM((TILE_T, d), jnp.bfloat16),
        ],
        compiler_params=pltpu.CompilerParams(
            dimension_semantics=("arbitrary", "arbitrary"),
            vmem_limit_bytes=VMEM_LIMIT_BYTES),
        name="mixer",
    )(x, mod, *consts)


def _ffn_kernel(x_ref, mod_ref, g2_ref, w1_ref, w2_ref, gf_ref, o_ref, *, hidden_tile, final_norm):
    x = x_ref[0]
    mod = mod_ref[...]
    hb = _bf16(_rms_mod(x, g2_ref[...], mod[4:5], mod[3:4]))
    hidden = w1_ref.shape[1]
    acc = jnp.zeros(x.shape, jnp.float32)
    for j in range(0, hidden, hidden_tile):
        a = jnp.maximum(_dot(hb, w1_ref[:, j:j + hidden_tile]), 0.0)
        acc = acc + _dot(_bf16(a * a), w2_ref[j:j + hidden_tile, :])
    y = x + mod[5:6] * acc
    if final_norm:
        ms = jnp.mean(y * y, axis=-1, keepdims=True)
        y = (y * lax.rsqrt(ms + EPS)) * gf_ref[...]
    o_ref[0] = y


def _ffn_call(layer, x, mod, g2, w1, w2, gf, final_norm):
    bsz, seq, d = x.shape
    hidden = w1.shape[2]
    kern = functools.partial(_ffn_kernel, hidden_tile=min(hidden, 1024), final_norm=final_norm)
    consts = (g2, w1, w2)
    return pl.pallas_call(
        kern,
        grid=(bsz, seq // TILE_T),
        in_specs=[
            pl.BlockSpec((1, TILE_T, d), lambda b, t: (b, t, 0)),
            pl.BlockSpec((None, None, N_MOD, d), lambda b, t: (layer, b, 0, 0)),
        ] + [_layer_spec(a.shape, layer) for a in consts] + [
            pl.BlockSpec(gf.shape, lambda b, t: (0, 0), pipeline_mode=pl.Buffered(1))],
        out_specs=pl.BlockSpec((1, TILE_T, d), lambda b, t: (b, t, 0)),
        out_shape=jax.ShapeDtypeStruct(x.shape, jnp.float32),
        compiler_params=pltpu.CompilerParams(
            dimension_semantics=("arbitrary", "arbitrary"),
            vmem_limit_bytes=VMEM_LIMIT_BYTES),
        name="ffn",
    )(x, mod, *consts, gf)


def kernel(x, c, w_ada, b_ada, norm1_g, w_in, a_ln_g, a_ln_b, a_ws, a_bs, w_pa,
           b_conv_w, b_conv_b, b_ln_g, b_ln_b, w_pb, w_out, norm2_g, w_ff1, w_ff2, final_g):
    depth, d = norm1_g.shape
    bsz = x.shape[0]
    heads = a_ws.shape[1]
    aw = a_ln_g.shape[1]
    bw = b_ln_g.shape[1]
    hd = aw // heads

    mod = _ada_call(c, w_ada, b_ada).reshape(depth, bsz, N_MOD, d)
    bs_full = jnp.repeat(jnp.swapaxes(a_bs, 1, 2), hd, axis=2)
    w_in_r = _bf16(jnp.concatenate(
        [w_in[:, :, 2 * aw:2 * aw + 2 * bw], w_in[:, :, :2 * aw], w_in[:, :, 2 * aw + 2 * bw:]],
        axis=2))
    rows = lambda a: a.reshape(depth, 1, -1)
    mixer_ops = (mod, rows(norm1_g), w_in_r, rows(a_ln_g), rows(a_ln_b), a_ws, bs_full, _bf16(w_pa),
                 b_conv_w[:, :, 0, :], rows(b_conv_b), rows(b_ln_g), rows(b_ln_b), _bf16(w_pb),
                 _bf16(w_out))
    ffn_ops = (mod, rows(norm2_g), _bf16(w_ff1), _bf16(w_ff2), final_g.reshape(1, -1))

    for l in range(depth):
        x = _mixer_call(l, x, *mixer_ops)
        x = _ffn_call(l, x, *ffn_ops, final_norm=(l == depth - 1))
    return x
```

```python
import functools

import jax
import jax.numpy as jnp
from jax import lax
from jax.experimental import pallas as pl
from jax.experimental.pallas import tpu as pltpu

EPS = 1e-6
N_MOD = 6
VMEM_LIMIT_BYTES = 56 * 1024 * 1024
SUBLANES = 8
LANES = 128
HALO = 32
TILE_T = 512
FFN_TILE_T = 1024
CONV_ROWS = 128


def _bf16(a):
    return a.astype(jnp.bfloat16)


def _dot(a, b):
    return jnp.dot(a, b, preferred_element_type=jnp.float32)


def _rms_mod(x, g, sc, sh):
    ms = jnp.mean(x * x, axis=-1, keepdims=True)
    return (x * lax.rsqrt(ms + EPS)) * (g * (1.0 + sc)) + sh


def _row_sum(p):
    hi = _bf16(p)
    lo = _bf16(p - hi.astype(jnp.float32))
    ones = jnp.ones((2 * LANES, LANES), jnp.bfloat16)
    return _dot(jnp.concatenate([hi, lo], axis=1), ones)


def _layernorm(v, g, b):
    n = v.shape[1] // LANES
    cols = [v[:, j * LANES:(j + 1) * LANES] for j in range(n)]
    inv = 1.0 / v.shape[1]
    mu = _row_sum(functools.reduce(lambda a, c: a + c, cols)) * inv
    cen = [c - mu for c in cols]
    var = _row_sum(functools.reduce(lambda a, c: a + c, [c * c for c in cen])) * inv
    rstd = lax.rsqrt(var + EPS)
    out = [cen[j] * rstd * g[:, j * LANES:(j + 1) * LANES] + b[:, j * LANES:(j + 1) * LANES]
           for j in range(n)]
    return jnp.concatenate(out, axis=1)


def _ada_kernel(c_ref, w_ref, b_ref, o_ref):
    c = c_ref[...]
    c_act = _bf16(c * jax.nn.sigmoid(c))
    o_ref[0] = _dot(c_act, _bf16(w_ref[0])) + b_ref[0]


def _ada_call(c, w_ada, b_ada):
    depth, d, ncols = w_ada.shape
    bsz = c.shape[0]
    col_tile = ncols // 3
    return pl.pallas_call(
        _ada_kernel,
        grid=(depth, ncols // col_tile),
        in_specs=[
            pl.BlockSpec((bsz, d), lambda l, j: (0, 0)),
            pl.BlockSpec((1, d, col_tile), lambda l, j: (l, 0, j)),
            pl.BlockSpec((1, 1, col_tile), lambda l, j: (l, 0, j)),
        ],
        out_specs=pl.BlockSpec((1, bsz, col_tile), lambda l, j: (l, 0, j)),
        out_shape=jax.ShapeDtypeStruct((depth, bsz, ncols), jnp.float32),
        compiler_params=pltpu.CompilerParams(
            dimension_semantics=("arbitrary", "arbitrary"),
            vmem_limit_bytes=VMEM_LIMIT_BYTES),
        name="adaln_mod",
    )(c, w_ada, b_ada.reshape(depth, 1, ncols))


def _mixer_kernel(x_ref, mod_ref, g1_ref, w_in_ref, alng_ref, alnb_ref, ws_ref, bs_ref,
                  w_pa_ref, cw_ref, cb_ref, blng_ref, blnb_ref, w_pb_ref, w_out_ref,
                  o_ref, zs_ref, cv_ref, pr_ref, hb_ref, *, heads, chunk, conv_width):
    t = pl.program_id(1)
    tile_t, d = x_ref.shape[1], x_ref.shape[2]
    aw = alng_ref.shape[1]
    bw = blng_ref.shape[1]
    hd = aw // heads

    @pl.when(t == 0)
    def _():
        zs_ref[0:HALO, :] = jnp.zeros((HALO, bw), jnp.float32)

    mod = mod_ref[...]
    hb_ref[...] = _bf16(_rms_mod(x_ref[0], g1_ref[...], mod[1:2], mod[0:1]))
    hb = hb_ref[...]

    pair = 2 * LANES
    for c0 in range(0, bw, pair):
        pa = _dot(hb, w_in_ref[:, 2 * aw + c0:2 * aw + c0 + pair])
        pg = _dot(hb, w_in_ref[:, 2 * aw + bw + c0:2 * aw + bw + c0 + pair])
        zs_ref[HALO:HALO + tile_t, c0:c0 + pair] = pa * jax.nn.sigmoid(pg)

    pr_ref[:, 0:2 * aw] = _dot(hb, w_in_ref[:, 0:2 * aw])
    pr_ref[:, 2 * aw:] = _dot(hb, w_in_ref[:, 2 * aw + 2 * bw:])

    first = HALO - (conv_width - 1)
    span = CONV_ROWS + HALO

    for c0 in range(0, bw, LANES):
        for i in range(tile_t // CONV_ROWS):
            base = i * CONV_ROWS
            zz = zs_ref[base:base + span, c0:c0 + LANES]
            acc = jnp.zeros((CONV_ROWS, LANES), jnp.float32) + cb_ref[:, c0:c0 + LANES]
            for r in range(SUBLANES):
                rows = CONV_ROWS if r == 0 else CONV_ROWS + SUBLANES
                p = None
                for q in range(span // SUBLANES):
                    k = SUBLANES * q + r - first
                    if 0 <= k < conv_width:
                        term = (cw_ref[k:k + 1, c0:c0 + LANES]
                                * zz[SUBLANES * q:SUBLANES * q + rows, :])
                        p = term if p is None else p + term
                if p is not None:
                    acc = acc + p[r:r + CONV_ROWS]
            cv_ref[base:base + CONV_ROWS, c0:c0 + LANES] = acc
    zs_ref[0:HALO, :] = zs_ref[tile_t:tile_t + HALO, :]

    zc = _layernorm(cv_ref[...], blng_ref[...], blnb_ref[...])
    y_b = _dot(_bf16(zc * jax.nn.sigmoid(zc)), w_pb_ref[...])

    vb = _bf16(_layernorm(pr_ref[:, aw:2 * aw], alng_ref[...], alnb_ref[...]))
    row = lax.broadcasted_iota(jnp.int32, (chunk, chunk), 0)
    col = lax.broadcasted_iota(jnp.int32, (chunk, chunk), 1)
    s_rows = []
    for ci in range(tile_t // chunk):
        s_heads = []
        for h in range(heads):
            w_h = _bf16(jnp.where(col <= row, ws_ref[h], 0.0))
            s_heads.append(_dot(w_h, vb[ci * chunk:(ci + 1) * chunk, h * hd:(h + 1) * hd]))
        s_rows.append(jnp.concatenate(s_heads, axis=1) + bs_ref[...])
    s = jnp.concatenate(s_rows, axis=0)
    y_a = _dot(_bf16(pr_ref[:, 0:aw] * s), w_pa_ref[...])

    g_a = pr_ref[:, 2 * aw:2 * aw + d]
    g_b = pr_ref[:, 2 * aw + d:2 * aw + 2 * d]
    merged = jax.nn.sigmoid(g_a) * y_a + jax.nn.sigmoid(g_b) * y_b
    o_ref[0] = x_ref[0] + mod[2:3] * _dot(_bf16(merged), w_out_ref[...])


def _layer_spec(shape, layer):
    nd = len(shape) - 1
    return pl.BlockSpec((None,) + tuple(shape[1:]), lambda b, t: (layer,) + (0,) * nd,
                        pipeline_mode=pl.Buffered(1))


def _mixer_call(layer, x, mod, g1, w_in, alng, alnb, ws, bs_full, w_pa, cw, cb, blng, blnb, w_pb, w_out):
    bsz, seq, d = x.shape
    heads, chunk = ws.shape[1], ws.shape[2]
    conv_width, bw = cw.shape[1], cw.shape[2]
    aw = alng.shape[2]
    assert seq % TILE_T == 0 and TILE_T % chunk == 0 and TILE_T % CONV_ROWS == 0
    assert conv_width - 1 <= HALO and bw % LANES == 0 and aw % LANES == 0
    kern = functools.partial(_mixer_kernel, heads=heads, chunk=chunk, conv_width=conv_width)
    consts = (g1, w_in, alng, alnb, ws, bs_full, w_pa, cw, cb, blng, blnb, w_pb, w_out)
    return pl.pallas_call(
        kern,
        grid=(bsz, seq // TILE_T),
        in_specs=[
            pl.BlockSpec((1, TILE_T, d), lambda b, t: (b, t, 0)),
            pl.BlockSpec((None, None, N_MOD, d), lambda b, t: (layer, b, 0, 0)),
        ] + [_layer_spec(a.shape, layer) for a in consts],
        out_specs=pl.BlockSpec((1, TILE_T, d), lambda b, t: (b, t, 0)),
        out_shape=jax.ShapeDtypeStruct(x.shape, jnp.float32),
        scratch_shapes=[
            pltpu.VMEM((HALO + TILE_T, bw), jnp.float32),
            pltpu.VMEM((TILE_T, bw), jnp.float32),
            pltpu.VMEM((TILE_T, 2 * aw + 2 * d), jnp.float32),
            pltpu.VMEM((TILE_T, d), jnp.bfloat16),
        ],
        compiler_params=pltpu.CompilerParams(
            dimension_semantics=("arbitrary", "arbitrary"),
            vmem_limit_bytes=VMEM_LIMIT_BYTES),
        name="mixer",
    )(x, mod, *consts)


def _ffn_kernel(x_ref, mod_ref, g2_ref, w1_ref, w2_ref, gf_ref, o_ref, *, hidden_tile, final_norm):
    x = x_ref[0]
    mod = mod_ref[...]
    hb = _bf16(_rms_mod(x, g2_ref[...], mod[4:5], mod[3:4]))
    hidden = w1_ref.shape[1]
    acc = jnp.zeros(x.shape, jnp.float32)
    for j in range(0, hidden, hidden_tile):
        a = jnp.maximum(_dot(hb, w1_ref[:, j:j + hidden_tile]), 0.0)
        acc = acc + _dot(_bf16(a * a), w2_ref[j:j + hidden_tile, :])
    y = x + mod[5:6] * acc
    if final_norm:
        ms = jnp.mean(y * y, axis=-1, keepdims=True)
        y = (y * lax.rsqrt(ms + EPS)) * gf_ref[...]
    o_ref[0] = y


def _ffn_call(layer, x, mod, g2, w1, w2, gf, final_norm):
    bsz, seq, d = x.shape
    hidden = w1.shape[2]
    kern = functools.partial(_ffn_kernel, hidden_tile=min(hidden, 1024), final_norm=final_norm)
    consts = (g2, w1, w2)
    tile = FFN_TILE_T
    assert seq % tile == 0
    return pl.pallas_call(
        kern,
        grid=(bsz, seq // tile),
        in_specs=[
            pl.BlockSpec((1, tile, d), lambda b, t: (b, t, 0)),
            pl.BlockSpec((None, None, N_MOD, d), lambda b, t: (layer, b, 0, 0)),
        ] + [_layer_spec(a.shape, layer) for a in consts] + [
            pl.BlockSpec(gf.shape, lambda b, t: (0, 0), pipeline_mode=pl.Buffered(1))],
        out_specs=pl.BlockSpec((1, tile, d), lambda b, t: (b, t, 0)),
        out_shape=jax.ShapeDtypeStruct(x.shape, jnp.float32),
        compiler_params=pltpu.CompilerParams(
            dimension_semantics=("arbitrary", "arbitrary"),
            vmem_limit_bytes=VMEM_LIMIT_BYTES),
        name="ffn",
    )(x, mod, *consts, gf)


def kernel(x, c, w_ada, b_ada, norm1_g, w_in, a_ln_g, a_ln_b, a_ws, a_bs, w_pa,
           b_conv_w, b_conv_b, b_ln_g, b_ln_b, w_pb, w_out, norm2_g, w_ff1, w_ff2, final_g):
    depth, d = norm1_g.shape
    bsz = x.shape[0]
    heads = a_ws.shape[1]
    aw = a_ln_g.shape[1]
    bw = b_ln_g.shape[1]
    hd = aw // heads

    mod = _ada_call(c, w_ada, b_ada).reshape(depth, bsz, N_MOD, d)
    bs_full = jnp.repeat(jnp.swapaxes(a_bs, 1, 2), hd, axis=2)
    rows = lambda a: a.reshape(depth, 1, -1)
    mixer_ops = (mod, rows(norm1_g), _bf16(w_in), rows(a_ln_g), rows(a_ln_b), a_ws, bs_full, _bf16(w_pa),
                 b_conv_w[:, :, 0, :], rows(b_conv_b), rows(b_ln_g), rows(b_ln_b), _bf16(w_pb),
                 _bf16(w_out))
    ffn_ops = (mod, rows(norm2_g), _bf16(w_ff1), _bf16(w_ff2), final_g.reshape(1, -1))

    for l in range(depth):
        x = _mixer_call(l, x, *mixer_ops)
        x = _ffn_call(l, x, *ffn_ops, final_norm=(l == depth - 1))
    return x
```

```python
import functools

import jax
import jax.numpy as jnp
from jax import lax
from jax.experimental import pallas as pl
from jax.experimental.pallas import tpu as pltpu

EPS = 1e-6
N_MOD = 6
VMEM_LIMIT_BYTES = 56 * 1024 * 1024
SUBLANES = 8
LANES = 128
HALO = 32
TILE_T = 512
FFN_TILE_T = 1024
CONV_ROWS = 256


def _bf16(a):
    return a.astype(jnp.bfloat16)


def _dot(a, b):
    return jnp.dot(a, b, preferred_element_type=jnp.float32)


def _rms_mod(x, g, sc, sh):
    ms = jnp.mean(x * x, axis=-1, keepdims=True)
    return (x * lax.rsqrt(ms + EPS)) * (g * (1.0 + sc)) + sh


def _row_sum(p):
    hi = _bf16(p)
    lo = _bf16(p - hi.astype(jnp.float32))
    ones = jnp.ones((2 * LANES, LANES), jnp.bfloat16)
    return _dot(jnp.concatenate([hi, lo], axis=1), ones)


def _layernorm(v, g, b):
    n = v.shape[1] // LANES
    cols = [v[:, j * LANES:(j + 1) * LANES] for j in range(n)]
    inv = 1.0 / v.shape[1]
    mu = _row_sum(functools.reduce(lambda a, c: a + c, cols)) * inv
    cen = [c - mu for c in cols]
    var = _row_sum(functools.reduce(lambda a, c: a + c, [c * c for c in cen])) * inv
    rstd = lax.rsqrt(var + EPS)
    out = [cen[j] * rstd * g[:, j * LANES:(j + 1) * LANES] + b[:, j * LANES:(j + 1) * LANES]
           for j in range(n)]
    return jnp.concatenate(out, axis=1)


def _ada_kernel(c_ref, w_ref, b_ref, o_ref):
    c = c_ref[...]
    c_act = _bf16(c * jax.nn.sigmoid(c))
    o_ref[0] = _dot(c_act, _bf16(w_ref[0])) + b_ref[0]


def _ada_call(c, w_ada, b_ada):
    depth, d, ncols = w_ada.shape
    bsz = c.shape[0]
    col_tile = ncols // 3
    return pl.pallas_call(
        _ada_kernel,
        grid=(depth, ncols // col_tile),
        in_specs=[
            pl.BlockSpec((bsz, d), lambda l, j: (0, 0)),
            pl.BlockSpec((1, d, col_tile), lambda l, j: (l, 0, j)),
            pl.BlockSpec((1, 1, col_tile), lambda l, j: (l, 0, j)),
        ],
        out_specs=pl.BlockSpec((1, bsz, col_tile), lambda l, j: (l, 0, j)),
        out_shape=jax.ShapeDtypeStruct((depth, bsz, ncols), jnp.float32),
        compiler_params=pltpu.CompilerParams(
            dimension_semantics=("arbitrary", "arbitrary"),
            vmem_limit_bytes=VMEM_LIMIT_BYTES),
        name="adaln_mod",
    )(c, w_ada, b_ada.reshape(depth, 1, ncols))


def _mixer_kernel(x_ref, mod_ref, g1_ref, w_in_ref, alng_ref, alnb_ref, ws_ref, bs_ref,
                  w_pa_ref, cw_ref, cb_ref, blng_ref, blnb_ref, w_pb_ref, w_out_ref,
                  o_ref, zs_ref, cv_ref, pr_ref, hb_ref, *, heads, chunk, conv_width):
    t = pl.program_id(1)
    tile_t, d = x_ref.shape[1], x_ref.shape[2]
    aw = alng_ref.shape[1]
    bw = blng_ref.shape[1]
    hd = aw // heads

    @pl.when(t == 0)
    def _():
        zs_ref[0:HALO, :] = jnp.zeros((HALO, bw), jnp.float32)

    mod = mod_ref[...]
    hb_ref[...] = _bf16(_rms_mod(x_ref[0], g1_ref[...], mod[1:2], mod[0:1]))
    hb = hb_ref[...]

    pair = 2 * LANES
    for c0 in range(0, bw, pair):
        pa = _dot(hb, w_in_ref[:, 2 * aw + c0:2 * aw + c0 + pair])
        pg = _dot(hb, w_in_ref[:, 2 * aw + bw + c0:2 * aw + bw + c0 + pair])
        zs_ref[HALO:HALO + tile_t, c0:c0 + pair] = pa * jax.nn.sigmoid(pg)

    pr_ref[:, 0:2 * aw] = _dot(hb, w_in_ref[:, 0:2 * aw])
    pr_ref[:, 2 * aw:] = _dot(hb, w_in_ref[:, 2 * aw + 2 * bw:])

    first = HALO - (conv_width - 1)
    span = CONV_ROWS + HALO

    for c0 in range(0, bw, LANES):
        for i in range(tile_t // CONV_ROWS):
            base = i * CONV_ROWS
            zz = zs_ref[base:base + span, c0:c0 + LANES]
            acc = jnp.zeros((CONV_ROWS, LANES), jnp.float32) + cb_ref[:, c0:c0 + LANES]
            for r in range(SUBLANES):
                rows = CONV_ROWS if r == 0 else CONV_ROWS + SUBLANES
                p = None
                for q in range(span // SUBLANES):
                    k = SUBLANES * q + r - first
                    if 0 <= k < conv_width:
                        term = (cw_ref[k:k + 1, c0:c0 + LANES]
                                * zz[SUBLANES * q:SUBLANES * q + rows, :])
                        p = term if p is None else p + term
                if p is not None:
                    acc = acc + p[r:r + CONV_ROWS]
            cv_ref[base:base + CONV_ROWS, c0:c0 + LANES] = acc
    zs_ref[0:HALO, :] = zs_ref[tile_t:tile_t + HALO, :]

    zc = _layernorm(cv_ref[...], blng_ref[...], blnb_ref[...])
    y_b = _dot(_bf16(zc * jax.nn.sigmoid(zc)), w_pb_ref[...])

    vb = _bf16(_layernorm(pr_ref[:, aw:2 * aw], alng_ref[...], alnb_ref[...]))
    row = lax.broadcasted_iota(jnp.int32, (chunk, chunk), 0)
    col = lax.broadcasted_iota(jnp.int32, (chunk, chunk), 1)
    s_rows = []
    for ci in range(tile_t // chunk):
        s_heads = []
        for h in range(heads):
            w_h = _bf16(jnp.where(col <= row, ws_ref[h], 0.0))
            s_heads.append(_dot(w_h, vb[ci * chunk:(ci + 1) * chunk, h * hd:(h + 1) * hd]))
        s_rows.append(jnp.concatenate(s_heads, axis=1) + bs_ref[...])
    s = jnp.concatenate(s_rows, axis=0)
    y_a = _dot(_bf16(pr_ref[:, 0:aw] * s), w_pa_ref[...])

    g_a = pr_ref[:, 2 * aw:2 * aw + d]
    g_b = pr_ref[:, 2 * aw + d:2 * aw + 2 * d]
    merged = jax.nn.sigmoid(g_a) * y_a + jax.nn.sigmoid(g_b) * y_b
    o_ref[0] = x_ref[0] + mod[2:3] * _dot(_bf16(merged), w_out_ref[...])


def _layer_spec(shape, layer):
    nd = len(shape) - 1
    return pl.BlockSpec((None,) + tuple(shape[1:]), lambda b, t: (layer,) + (0,) * nd,
                        pipeline_mode=pl.Buffered(1))


def _mixer_call(layer, x, mod, g1, w_in, alng, alnb, ws, bs_full, w_pa, cw, cb, blng, blnb, w_pb, w_out):
    bsz, seq, d = x.shape
    heads, chunk = ws.shape[1], ws.shape[2]
    conv_width, bw = cw.shape[1], cw.shape[2]
    aw = alng.shape[2]
    assert seq % TILE_T == 0 and TILE_T % chunk == 0 and TILE_T % CONV_ROWS == 0
    assert conv_width - 1 <= HALO and bw % LANES == 0 and aw % LANES == 0
    kern = functools.partial(_mixer_kernel, heads=heads, chunk=chunk, conv_width=conv_width)
    consts = (g1, w_in, alng, alnb, ws, bs_full, w_pa, cw, cb, blng, blnb, w_pb, w_out)
    return pl.pallas_call(
        kern,
        grid=(bsz, seq // TILE_T),
        in_specs=[
            pl.BlockSpec((1, TILE_T, d), lambda b, t: (b, t, 0)),
            pl.BlockSpec((None, None, N_MOD, d), lambda b, t: (layer, b, 0, 0)),
        ] + [_layer_spec(a.shape, layer) for a in consts],
        out_specs=pl.BlockSpec((1, TILE_T, d), lambda b, t: (b, t, 0)),
        out_shape=jax.ShapeDtypeStruct(x.shape, jnp.float32),
        scratch_shapes=[
            pltpu.VMEM((HALO + TILE_T, bw), jnp.float32),
            pltpu.VMEM((TILE_T, bw), jnp.float32),
            pltpu.VMEM((TILE_T, 2 * aw + 2 * d), jnp.float32),
            pltpu.VMEM((TILE_T, d), jnp.bfloat16),
        ],
        compiler_params=pltpu.CompilerParams(
            dimension_semantics=("arbitrary", "arbitrary"),
            vmem_limit_bytes=VMEM_LIMIT_BYTES),
        name="mixer",
    )(x, mod, *consts)


def _ffn_kernel(x_ref, mod_ref, g2_ref, w1_ref, w2_ref, gf_ref, o_ref, *, hidden_tile, final_norm):
    x = x_ref[0]
    mod = mod_ref[...]
    hb = _bf16(_rms_mod(x, g2_ref[...], mod[4:5], mod[3:4]))
    hidden = w1_ref.shape[1]
    acc = jnp.zeros(x.shape, jnp.float32)
    for j in range(0, hidden, hidden_tile):
        a = jnp.maximum(_dot(hb, w1_ref[:, j:j + hidden_tile]), 0.0)
        acc = acc + _dot(_bf16(a * a), w2_ref[j:j + hidden_tile, :])
    y = x + mod[5:6] * acc
    if final_norm:
        ms = jnp.mean(y * y, axis=-1, keepdims=True)
        y = (y * lax.rsqrt(ms + EPS)) * gf_ref[...]
    o_ref[0] = y


def _ffn_call(layer, x, mod, g2, w1, w2, gf, final_norm):
    bsz, seq, d = x.shape
    hidden = w1.shape[2]
    kern = functools.partial(_ffn_kernel, hidden_tile=min(hidden, 1024), final_norm=final_norm)
    consts = (g2, w1, w2)
    tile = FFN_TILE_T
    assert seq % tile == 0
    return pl.pallas_call(
        kern,
        grid=(bsz, seq // tile),
        in_specs=[
            pl.BlockSpec((1, tile, d), lambda b, t: (b, t, 0)),
            pl.BlockSpec((None, None, N_MOD, d), lambda b, t: (layer, b, 0, 0)),
        ] + [_layer_spec(a.shape, layer) for a in consts] + [
            pl.BlockSpec(gf.shape, lambda b, t: (0, 0), pipeline_mode=pl.Buffered(1))],
        out_specs=pl.BlockSpec((1, tile, d), lambda b, t: (b, t, 0)),
        out_shape=jax.ShapeDtypeStruct(x.shape, jnp.float32),
        compiler_params=pltpu.CompilerParams(
            dimension_semantics=("arbitrary", "arbitrary"),
            vmem_limit_bytes=VMEM_LIMIT_BYTES),
        name="ffn",
    )(x, mod, *consts, gf)


def kernel(x, c, w_ada, b_ada, norm1_g, w_in, a_ln_g, a_ln_b, a_ws, a_bs, w_pa,
           b_conv_w, b_conv_b, b_ln_g, b_ln_b, w_pb, w_out, norm2_g, w_ff1, w_ff2, final_g):
    depth, d = norm1_g.shape
    bsz = x.shape[0]
    heads = a_ws.shape[1]
    aw = a_ln_g.shape[1]
    bw = b_ln_g.shape[1]
    hd = aw // heads

    mod = _ada_call(c, w_ada, b_ada).reshape(depth, bsz, N_MOD, d)
    bs_full = jnp.repeat(jnp.swapaxes(a_bs, 1, 2), hd, axis=2)
    rows = lambda a: a.reshape(depth, 1, -1)
    mixer_ops = (mod, rows(norm1_g), _bf16(w_in), rows(a_ln_g), rows(a_ln_b), a_ws, bs_full, _bf16(w_pa),
                 b_conv_w[:, :, 0, :], rows(b_conv_b), rows(b_ln_g), rows(b_ln_b), _bf16(w_pb),
                 _bf16(w_out))
    ffn_ops = (mod, rows(norm2_g), _bf16(w_ff1), _bf16(w_ff2), final_g.reshape(1, -1))

    for l in range(depth):
        x = _mixer_call(l, x, *mixer_ops)
        x = _ffn_call(l, x, *ffn_ops, final_norm=(l == depth - 1))
    return x
```
